```python
import jax, jax.numpy as jnp
from jax import lax
import numpy as np

D_MODEL = 2048
BATCH = 2
SEQ = 8192
DEPTH = 1

D_CONV = D_MODEL // 2
CONV_GROUPS = 8
CONV_K = 31
D_SGU = D_MODEL // 2
SGU_GROUPS = 8
SGU_HEAD = D_SGU // SGU_GROUPS
CHUNK = 128
D_FF = 5632
FFN_K = 3
N_MOD = 6
D_IN = 2 * D_CONV + 2 * D_SGU + 2 * D_MODEL
EPS = 1e-6

kernel_name = "hybrid_conformer_gmlp_convffn_adaln"


def rms_norm(x, g):
    xf = x.astype(jnp.float32)
    y = xf * lax.rsqrt(jnp.mean(xf * xf, axis=-1, keepdims=True) + EPS)
    return (y * g.astype(jnp.float32)).astype(x.dtype)


def layer_norm(x, g, b):
    xf = x.astype(jnp.float32)
    mu = jnp.mean(xf, axis=-1, keepdims=True)
    var = jnp.mean(jnp.square(xf - mu), axis=-1, keepdims=True)
    y = (xf - mu) * lax.rsqrt(var + EPS)
    return (y * g.astype(jnp.float32) + b.astype(jnp.float32)).astype(x.dtype)


def causal_dwconv(x, w, b):
    k, ch = w.shape
    y = lax.conv_general_dilated(
        x, w.astype(x.dtype)[:, None, :], window_strides=(1,), padding=[(k - 1, 0)],
        dimension_numbers=("NWC", "WIO", "NWC"), feature_group_count=ch)
    return y + b.astype(x.dtype)


def setup_inputs(seed: int = 0) -> dict:
    key = jax.random.key(seed)
    ks = jax.random.split(key, 24)
    L, D = DEPTH, D_MODEL
    n = lambda k, shape, s: jax.random.normal(k, shape, jnp.float32) * s
    return {
        "x": n(ks[0], (BATCH, SEQ, D), 1.0),
        "c": n(ks[1], (BATCH, D), 1.0),
        "w_ada": n(ks[2], (L, D, N_MOD * D), 0.5 * D ** -0.5),
        "b_ada": n(ks[3], (L, N_MOD * D), 0.01),
        "norm1_g": 1.0 + n(ks[4], (L, D), 0.02),
        "w_in": n(ks[5], (L, D, D_IN), D ** -0.5),
        "b_in": n(ks[6], (L, D_IN), 0.01),
        "conv_dw_w": n(ks[7], (L, CONV_K, D_CONV), CONV_K ** -0.5),
        "conv_dw_b": n(ks[8], (L, D_CONV), 0.01),
        "conv_ln_g": 1.0 + n(ks[9], (L, D_CONV), 0.02),
        "conv_ln_b": n(ks[10], (L, D_CONV), 0.01),
        "w_conv_out": n(ks[11], (L, D_CONV, D), D_CONV ** -0.5),
        "sgu_ln_g": 1.0 + n(ks[12], (L, D_SGU), 0.02),
        "sgu_ln_b": n(ks[13], (L, D_SGU), 0.01),
        "w_spatial": n(ks[14], (L, SGU_GROUPS, CHUNK, CHUNK), CHUNK ** -0.5),
        "b_spatial": 1.0 + n(ks[15], (L, SGU_GROUPS, CHUNK), 0.01),
        "w_sgu_out": n(ks[16], (L, D_SGU, D), D_SGU ** -0.5),
        "w_out": n(ks[17], (L, D, D), D ** -0.5),
        "norm2_g": 1.0 + n(ks[18], (L, D), 0.02),
        "w_up": n(ks[19], (L, D, 2 * D_FF), D ** -0.5),
        "ffn_dw_w": n(ks[20], (L, FFN_K, 2 * D_FF), FFN_K ** -0.5),
        "ffn_dw_b": n(ks[21], (L, 2 * D_FF), 0.01),
        "w_down": n(ks[22], (L, D_FF, D), D_FF ** -0.5),
        "final_g": 1.0 + n(ks[23], (D,), 0.02),
    }


def reference(x, c, w_ada, b_ada, norm1_g, w_in, b_in, conv_dw_w, conv_dw_b, conv_ln_g,
              conv_ln_b, w_conv_out, sgu_ln_g, sgu_ln_b, w_spatial, b_spatial, w_sgu_out,
              w_out, norm2_g, w_up, ffn_dw_w, ffn_dw_b, w_down, final_g):
    B, S, D = x.shape
    n_chunks = S // CHUNK
    causal_mask = jnp.tril(jnp.ones((CHUNK, CHUNK), dtype=x.dtype))
    c_act = jax.nn.silu(c)
    for l in range(DEPTH):
        mod = (c_act @ w_ada[l] + b_ada[l])[:, None, :]
        shift1, scale1, gate1, shift2, scale2, gate2 = jnp.split(mod, N_MOD, axis=-1)

        h = rms_norm(x, norm1_g[l]) * (1.0 + scale1) + shift1
        proj = h @ w_in[l] + b_in[l]
        a_in, s_in, gates = jnp.split(proj, [2 * D_CONV, 2 * D_CONV + 2 * D_SGU], axis=-1)

        a_val, a_gate = jnp.split(a_in, 2, axis=-1)
        a = a_val * jax.nn.sigmoid(a_gate)
        a = causal_dwconv(a, conv_dw_w[l], conv_dw_b[l])
        a = jax.nn.silu(layer_norm(a, conv_ln_g[l], conv_ln_b[l]))
        y_a = a @ w_conv_out[l]

        z = jax.nn.gelu(s_in, approximate=False)
        u, v = jnp.split(z, 2, axis=-1)
        v = layer_norm(v, sgu_ln_g[l], sgu_ln_b[l])
        v = v.reshape(B, n_chunks, CHUNK, SGU_GROUPS, SGU_HEAD)
        ws = w_spatial[l] * causal_mask
        v = jnp.einsum("gts,bnsgc->bntgc", ws, v) + b_spatial[l].T[:, :, None]
        y_b = (u * v.reshape(B, S, D_SGU)) @ w_sgu_out[l]

        g_a, g_b = jnp.split(gates, 2, axis=-1)
        merged = jax.nn.sigmoid(g_a) * y_a + jax.nn.sigmoid(g_b) * y_b
        x = x + gate1 * (merged @ w_out[l])

        h = rms_norm(x, norm2_g[l]) * (1.0 + scale2) + shift2
        up = causal_dwconv(h @ w_up[l], ffn_dw_w[l], ffn_dw_b[l])
        val, gt = jnp.split(up, 2, axis=-1)
        x = x + gate2 * ((jax.nn.silu(gt) * val) @ w_down[l])

    return rms_norm(x, final_g)
```

```python
import functools

import jax
import jax.numpy as jnp
from jax import lax
from jax.experimental import pallas as pl
from jax.experimental.pallas import tpu as pltpu

EPS = 1e-6
CHUNK = 128
SGU_GROUPS = 8
CONV_K = 31
CONV_HALO = 32
FFN_K = 3
FFN_HALO = 8
N_MOD = 6
SUBLANES = 8
LANES = 128

TM = 512
ADA_TN = 1024
FFN_FC = 512
VMEM_LIMIT = 56 * 1024 * 1024

_ARB2 = ("arbitrary", "arbitrary")


def _rms_scale(xf):
    return lax.rsqrt(jnp.mean(xf * xf, axis=-1, keepdims=True) + EPS)


def _layer_norm(xf, g, b):
    mu = jnp.mean(xf, axis=-1, keepdims=True)
    xc = xf - mu
    var = jnp.mean(xc * xc, axis=-1, keepdims=True)
    return xc * lax.rsqrt(var + EPS) * g + b


def _sigmoid(x):
    return 1.0 / (1.0 + jnp.exp(-x))


def _gelu_exact(x):
    return 0.5 * x * (1.0 + lax.erf(x * (2.0 ** -0.5)))


def _adaln_kernel(ct_ref, w_ref, b_ref, o_ref):
    ct = ct_ref[...]
    ca = ct * _sigmoid(ct)
    w = w_ref[...]
    for b in range(o_ref.shape[0]):
        o_ref[b:b + 1, :] = jnp.sum(ca[:, b:b + 1] * w, axis=0, keepdims=True) + b_ref[...]


def _adaln(c, w_ada, b_ada):
    bsz, d = c.shape
    n = w_ada.shape[1]
    return pl.pallas_call(
        _adaln_kernel,
        grid=(n // ADA_TN,),
        in_specs=[
            pl.BlockSpec((d, bsz), lambda j: (0, 0)),
            pl.BlockSpec((d, ADA_TN), lambda j: (0, j)),
            pl.BlockSpec((1, ADA_TN), lambda j: (0, j)),
        ],
        out_specs=pl.BlockSpec((bsz, ADA_TN), lambda j: (0, j)),
        out_shape=jax.ShapeDtypeStruct((bsz, n), jnp.float32),
        compiler_params=pltpu.CompilerParams(
            dimension_semantics=("arbitrary",), vmem_limit_bytes=VMEM_LIMIT),
        name="adaln",
    )(c.T, w_ada, b_ada.reshape(1, n))


def _causal_conv31(abuf, cw_ref, cb_ref, tm, width):
    outs = []
    for c0 in range(0, width, LANES):
        ext = abuf[:, c0:c0 + LANES]
        acc = jnp.broadcast_to(cb_ref[:, c0:c0 + LANES], (tm, LANES))
        for s in range(SUBLANES):
            rolled = ext if s == 0 else pltpu.roll(ext, s, 0)
            for q in range((CONV_K - 1 - s) // SUBLANES + 1):
                d = SUBLANES * q + s
                start = CONV_HALO - SUBLANES * q
                k = CONV_K - 1 - d
                acc = acc + cw_ref[k:k + 1, c0:c0 + LANES] * rolled[start:start + tm, :]
        outs.append(acc)
    return jnp.concatenate(outs, axis=1)


def _mix_in_kernel(tiles_per_seq, x_ref, mod_ref, g1_ref, w_ref, b_ref, cw_ref, cb_ref, clg_ref, clb_ref,
                   slg_ref, slb_ref, wsp_ref, bsp_ref,
                   a_ref, uv_ref, sga_ref, sgb_ref, h_scr, abuf):
    i = pl.program_id(0)
    j = pl.program_id(1)
    tm = x_ref.shape[0]
    half = w_ref.shape[1] // 2

    @pl.when(j == 0)
    def _():
        xf = x_ref[...]
        y = xf * _rms_scale(xf) * g1_ref[...]
        h_scr[...] = (y * (1.0 + mod_ref[1:2, :]) + mod_ref[0:1, :]).astype(h_scr.dtype)

    proj = jnp.dot(h_scr[...], w_ref[...], preferred_element_type=jnp.float32) + b_ref[...]

    @pl.when(j == 0)
    def _():
        @pl.when(i % tiles_per_seq == 0)
        def _():
            abuf[0:CONV_HALO, :] = jnp.zeros((CONV_HALO, half), jnp.float32)

        abuf[CONV_HALO:CONV_HALO + tm, :] = proj[:, :half] * _sigmoid(proj[:, half:])
        conv = _causal_conv31(abuf, cw_ref, cb_ref, tm, half)
        abuf[0:CONV_HALO, :] = abuf[tm:tm + CONV_HALO, :]
        ln = _layer_norm(conv, clg_ref[...], clb_ref[...])
        a_ref[...] = (ln * _sigmoid(ln)).astype(a_ref.dtype)

    @pl.when(j == 1)
    def _():
        z = _gelu_exact(proj)
        u = z[:, :half]
        v = _layer_norm(z[:, half:], slg_ref[...], slb_ref[...]).astype(jnp.bfloat16)
        n_chunks = tm // CHUNK
        row = lax.broadcasted_iota(jnp.int32, (CHUNK, CHUNK), 0)
        col = lax.broadcasted_iota(jnp.int32, (CHUNK, CHUNK), 1)
        for g in range(SGU_GROUPS):
            c0 = g * CHUNK
            ws = jnp.where(col <= row, wsp_ref[g], 0.0).astype(jnp.bfloat16)
            vg = jnp.concatenate([v[n * CHUNK:(n + 1) * CHUNK, c0:c0 + CHUNK] for n in range(n_chunks)], axis=1)
            mixed = jnp.dot(ws, vg, preferred_element_type=jnp.float32) + bsp_ref[g]
            for n in range(n_chunks):
                r0 = n * CHUNK
                uv_ref[r0:r0 + CHUNK, c0:c0 + CHUNK] = (
                    u[r0:r0 + CHUNK, c0:c0 + CHUNK] * mixed[:, r0:r0 + CHUNK]).astype(uv_ref.dtype)

    @pl.when(j == 2)
    def _():
        sga_ref[...] = _sigmoid(proj).astype(sga_ref.dtype)

    @pl.when(j == 3)
    def _():
        sgb_ref[...] = _sigmoid(proj).astype(sgb_ref.dtype)


def _mix_in(x2d, mod, g1, w_in, b_in, cw, cb, clg, clb, slg, slb, wsp, bsp, seq):
    t, d = x2d.shape
    d_in = w_in.shape[1]
    tn = d_in // 4
    half = tn // 2
    tiles_per_seq = seq // TM
    row = lambda i, j: (i, 0)
    const2 = lambda i, j: (0, 0)
    const3 = lambda i, j: (0, 0, 0)
    out_bf = lambda n: jax.ShapeDtypeStruct((t, n), jnp.bfloat16)
    return pl.pallas_call(
        functools.partial(_mix_in_kernel, tiles_per_seq),
        grid=(t // TM, 4),
        in_specs=[
            pl.BlockSpec((TM, d), row),
            pl.BlockSpec((None, N_MOD, d), lambda i, j: (i // tiles_per_seq, 0, 0)),
            pl.BlockSpec((1, d), const2),
            pl.BlockSpec((d, tn), lambda i, j: (0, j)),
            pl.BlockSpec((1, tn), lambda i, j: (0, j)),
            pl.BlockSpec((CONV_K, half), const2),
            pl.BlockSpec((1, half), const2),
            pl.BlockSpec((1, half), const2),
            pl.BlockSpec((1, half), const2),
            pl.BlockSpec((1, half), const2),
            pl.BlockSpec((1, half), const2),
            pl.BlockSpec((SGU_GROUPS, CHUNK, CHUNK), const3),
            pl.BlockSpec((SGU_GROUPS, CHUNK, 1), const3),
        ],
        out_specs=[
            pl.BlockSpec((TM, half), row),
            pl.BlockSpec((TM, half), row),
            pl.BlockSpec((TM, tn), row),
            pl.BlockSpec((TM, tn), row),
        ],
        out_shape=[out_bf(half), out_bf(half), out_bf(tn), out_bf(tn)],
        scratch_shapes=[
            pltpu.VMEM((TM, d), jnp.bfloat16),
            pltpu.VMEM((CONV_HALO + TM, half), jnp.float32),
        ],
        compiler_params=pltpu.CompilerParams(dimension_semantics=_ARB2, vmem_limit_bytes=VMEM_LIMIT),
        name="mix_in",
    )(x2d, mod, g1, w_in, b_in, cw, cb, clg, clb, slg, slb, wsp, bsp)


def _mix_out_kernel(a_ref, uv_ref, sga_ref, sgb_ref, x_ref, mod_ref, g2_ref, wc_ref, ws_ref, wo_ref,
                    x1_ref, h2_ref, merged_scr, x1lo_scr):
    j = pl.program_id(1)
    tn = wc_ref.shape[1]

    def merge(col0):
        ya = jnp.dot(a_ref[...], wc_ref[...], preferred_element_type=jnp.float32)
        yb = jnp.dot(uv_ref[...], ws_ref[...], preferred_element_type=jnp.float32)
        m = sga_ref[...].astype(jnp.float32) * ya + sgb_ref[...].astype(jnp.float32) * yb
        merged_scr[:, col0:col0 + tn] = m.astype(merged_scr.dtype)

    def project(col0):
        out = jnp.dot(merged_scr[...], wo_ref[...], preferred_element_type=jnp.float32)
        return x_ref[...] + mod_ref[2:3, col0:col0 + tn] * out

    @pl.when(j == 0)
    def _():
        merge(0)

    @pl.when(j == 1)
    def _():
        merge(tn)

    @pl.when(j == 2)
    def _():
        x1 = project(0)
        x1_ref[...] = x1
        x1lo_scr[...] = x1

    @pl.when(j == 3)
    def _():
        x1hi = project(tn)
        x1_ref[...] = x1hi
        x1lo = x1lo_scr[...]
        ssq = jnp.sum(x1lo * x1lo, axis=-1, keepdims=True) + jnp.sum(x1hi * x1hi, axis=-1, keepdims=True)
        rs = lax.rsqrt(ssq / (2 * tn) + EPS)
        for col0, part in ((0, x1lo), (tn, x1hi)):
            sl = slice(col0, col0 + tn)
            y = part * rs * g2_ref[:, sl]
            h2_ref[:, sl] = (y * (1.0 + mod_ref[4:5, sl]) + mod_ref[3:4, sl]).astype(h2_ref.dtype)


def _mix_out(a, uv, sga, sgb, x2d, mod, g2, wc, ws, wo, seq):
    t, d = x2d.shape
    dc = a.shape[1]
    tn = d // 2
    tiles_per_seq = seq // TM
    row = lambda i, j: (i, 0)
    lo = lambda i, j: (i, jnp.minimum(j, 1))
    hi = lambda i, j: (i, jnp.maximum(j - 2, 0))
    return pl.pallas_call(
        _mix_out_kernel,
        grid=(t // TM, 4),
        in_specs=[
            pl.BlockSpec((TM, dc), row),
            pl.BlockSpec((TM, dc), row),
            pl.BlockSpec((TM, tn), lo),
            pl.BlockSpec((TM, tn), lo),
            pl.BlockSpec((TM, tn), hi),
            pl.BlockSpec((None, N_MOD, d), lambda i, j: (i // tiles_per_seq, 0, 0)),
            pl.BlockSpec((1, d), lambda i, j: (0, 0)),
            pl.BlockSpec((dc, tn), lambda i, j: (0, jnp.minimum(j, 1))),
            pl.BlockSpec((dc, tn), lambda i, j: (0, jnp.minimum(j, 1))),
            pl.BlockSpec((d, tn), lambda i, j: (0, jnp.maximum(j - 2, 0))),
        ],
        out_specs=[
            pl.BlockSpec((TM, tn), hi),
            pl.BlockSpec((TM, d), row),
        ],
        out_shape=[jax.ShapeDtypeStruct((t, d), jnp.float32), jax.ShapeDtypeStruct((t, d), jnp.bfloat16)],
        scratch_shapes=[
            pltpu.VMEM((TM, d), jnp.bfloat16),
            pltpu.VMEM((TM, tn), jnp.float32),
        ],
        compiler_params=pltpu.CompilerParams(dimension_semantics=_ARB2, vmem_limit_bytes=VMEM_LIMIT),
        name="mix_out",
    )(a, uv, sga, sgb, x2d, mod, g2, wc, ws, wo)


def _causal_conv3(up, halo, w_ref, b_ref):
    tm = up.shape[0]
    w0, w1, w2 = w_ref[0:1, :], w_ref[1:2, :], w_ref[2:3, :]
    body = b_ref[...] + w2 * up + w1 * pltpu.roll(up, 1, 0) + w0 * pltpu.roll(up, 2, 0)
    head = jnp.concatenate([halo, up[:SUBLANES]], axis=0)
    h1 = pltpu.roll(head, 1, 0)[FFN_HALO:]
    h2 = pltpu.roll(head, 2, 0)[FFN_HALO:]
    first = b_ref[...] + w2 * up[:SUBLANES] + w1 * h1 + w0 * h2
    return jnp.concatenate([first, body[SUBLANES:]], axis=0)


def _ffn_kernel(tiles_per_seq, h2_ref, wv_ref, wg_ref, cwv_ref, cwg_ref, cbv_ref, cbg_ref, wd_ref, x1_ref,
                mod_ref, fg_ref, y_ref, acc_scr, halo_scr):
    i = pl.program_id(0)
    j = pl.program_id(1)
    nj = pl.num_programs(1)
    tm = h2_ref.shape[0]
    fc = wv_ref.shape[1]

    @pl.when(i % tiles_per_seq == 0)
    def _():
        halo_scr[j] = jnp.zeros((FFN_HALO, 2 * fc), jnp.float32)

    h2 = h2_ref[...]
    upv = jnp.dot(h2, wv_ref[...], preferred_element_type=jnp.float32)
    upg = jnp.dot(h2, wg_ref[...], preferred_element_type=jnp.float32)
    halo = halo_scr[j]
    val = _causal_conv3(upv, halo[:, :fc], cwv_ref, cbv_ref)
    gt = _causal_conv3(upg, halo[:, fc:], cwg_ref, cbg_ref)
    halo_scr[j] = jnp.concatenate([upv[tm - FFN_HALO:], upg[tm - FFN_HALO:]], axis=1)
    act = (gt * _sigmoid(gt) * val).astype(jnp.bfloat16)
    part = jnp.dot(act, wd_ref[...], preferred_element_type=jnp.float32)

    @pl.when(j == 0)
    def _():
        acc_scr[...] = part

    @pl.when(j > 0)
    def _():
        acc_scr[...] += part

    @pl.when(j == nj - 1)
    def _():
        x2 = x1_ref[...] + mod_ref[5:6, :] * acc_scr[...]
        y_ref[...] = x2 * _rms_scale(x2) * fg_ref[...]


def _ffn(h2, w_up, cw, cb, w_down, x1, mod, fg, seq):
    t, d = x1.shape
    dff = w_down.shape[0]
    nj = dff // FFN_FC
    tiles_per_seq = seq // TM
    row = lambda i, j: (i, 0)
    vcol = lambda i, j: (0, j)
    gcol = lambda i, j: (0, nj + j)
    return pl.pallas_call(
        functools.partial(_ffn_kernel, tiles_per_seq),
        grid=(t // TM, nj),
        in_specs=[
            pl.BlockSpec((TM, d), row),
            pl.BlockSpec((d, FFN_FC), vcol),
            pl.BlockSpec((d, FFN_FC), gcol),
            pl.BlockSpec((FFN_K, FFN_FC), vcol),
            pl.BlockSpec((FFN_K, FFN_FC), gcol),
            pl.BlockSpec((1, FFN_FC), vcol),
            pl.BlockSpec((1, FFN_FC), gcol),
            pl.BlockSpec((FFN_FC, d), lambda i, j: (j, 0)),
            pl.BlockSpec((TM, d), row),
            pl.BlockSpec((None, N_MOD, d), lambda i, j: (i // tiles_per_seq, 0, 0)),
            pl.BlockSpec((1, d), lambda i, j: (0, 0)),
        ],
        out_specs=pl.BlockSpec((TM, d), row),
        out_shape=jax.ShapeDtypeStruct((t, d), jnp.float32),
        scratch_shapes=[
            pltpu.VMEM((TM, d), jnp.float32),
            pltpu.VMEM((nj, FFN_HALO, 2 * FFN_FC), jnp.float32),
        ],
        compiler_params=pltpu.CompilerParams(dimension_semantics=_ARB2, vmem_limit_bytes=VMEM_LIMIT),
        name="ffn",
    )(h2, w_up, w_up, cw, cw, cb, cb, w_down, x1, mod, fg)


def kernel(x, c, w_ada, b_ada, norm1_g, w_in, b_in, conv_dw_w, conv_dw_b, conv_ln_g, conv_ln_b, w_conv_out,
           sgu_ln_g, sgu_ln_b, w_spatial, b_spatial, w_sgu_out, w_out, norm2_g, w_up, ffn_dw_w, ffn_dw_b,
           w_down, final_g):
    bsz, seq, d = x.shape
    depth = w_ada.shape[0]
    assert depth == 1, "ffn fuses the final RMSNorm, so only a single layer is supported"
    bf = jnp.bfloat16
    row = lambda v: v.reshape(1, -1)
    xs = x.reshape(bsz * seq, d)
    for l in range(depth):
        mod = _adaln(c, w_ada[l], b_ada[l]).reshape(bsz, N_MOD, d)
        a, uv, sga, sgb = _mix_in(
            xs, mod, row(norm1_g[l]), w_in[l].astype(bf), row(b_in[l]), conv_dw_w[l], row(conv_dw_b[l]),
            row(conv_ln_g[l]), row(conv_ln_b[l]), row(sgu_ln_g[l]), row(sgu_ln_b[l]), w_spatial[l],
            b_spatial[l][:, :, None], seq)
        x1, h2 = _mix_out(a, uv, sga, sgb, xs, mod, row(norm2_g[l]), w_conv_out[l].astype(bf),
                          w_sgu_out[l].astype(bf), w_out[l].astype(bf), seq)
        xs = _ffn(h2, w_up[l].astype(bf), ffn_dw_w[l], row(ffn_dw_b[l]), w_down[l].astype(bf), x1, mod,
                  row(final_g), seq)
    return xs.reshape(bsz, seq, d)
```
